```python
import math
import jax, jax.numpy as jnp
from jax import lax
import numpy as np

D_MODEL = 1024
BATCH = 16
SEQ = 2048
DEPTH = 2
DEC_BATCH = 32
DEC_SEQ = 2048
PAST_LEN = 128

GRID_W = 64
ROPE_THETA = 10000.0
EPS = 1e-6
GROUP_W = D_MODEL // 4
D_MIX = 4 * GROUP_W
A_HEADS = 4
A_KV_HEADS = 2
A_HEAD_DIM = GROUP_W // A_HEADS
Q_BLOCK = 128
B_HEADS = 4
B_NOPE = 64
B_ROPE = 32
B_V = GROUP_W // B_HEADS
KV_RANK = 128
C_HEADS = 4
C_QK = 32
C_V = GROUP_W // C_HEADS
RET_CHUNK = 128
D_HEADS = 4
D_K = 64
D_V = GROUP_W // D_HEADS
CONV_K = 5
CONV_CH = D_HEADS * (2 * D_K + D_V)
DELTA_CHUNK = 64
D_FF = 3584
N_EXPERTS = 8
TOP_K = 2
D_FF_EXPERT = 3584
N_DENSE = (DEPTH + 1) // 2
N_MOE = DEPTH // 2
SPLIT_SIZES = (
    A_HEADS * A_HEAD_DIM, A_KV_HEADS * A_HEAD_DIM, A_KV_HEADS * A_HEAD_DIM,
    B_HEADS * (B_NOPE + B_ROPE), KV_RANK, B_ROPE,
    C_HEADS * C_QK, C_HEADS * C_QK, C_HEADS * C_V, C_HEADS * C_V,
    D_HEADS * D_K, D_HEADS * D_K, D_HEADS * D_V, D_HEADS * D_V,
    2 * D_HEADS, 2 * D_HEADS,
)
D_IN = sum(SPLIT_SIZES)

kernel_name = 'hybrid_parallel_head_encoder'

F32 = jnp.float32


def rms_norm(x, g):
    xf = x.astype(F32)
    y = xf * lax.rsqrt(jnp.mean(xf * xf, axis=-1, keepdims=True) + EPS)
    return (y * g.astype(F32)).astype(x.dtype)


def l2_norm(x):
    return x * lax.rsqrt(jnp.sum(x * x, axis=-1, keepdims=True) + EPS)


def split_cols(p):
    parts, off = [], 0
    for size in SPLIT_SIZES:
        parts.append(p[..., off:off + size])
        off += size
    return parts


def axial_rope_tables(n_tokens, d_rot):
    rows = n_tokens // GRID_W
    n_f = d_rot // 4
    freqs = ROPE_THETA ** (-jnp.arange(n_f, dtype=F32) / n_f)
    row = jnp.repeat(jnp.arange(rows, dtype=F32), GRID_W)
    col = jnp.tile(jnp.arange(GRID_W, dtype=F32), rows)
    ang = jnp.stack([row[:, None] * freqs, col[:, None] * freqs], axis=1)
    return jnp.cos(ang), jnp.sin(ang)


def apply_axial_rope(x, cos, sin):
    b, s, h, d = x.shape
    xr = x.astype(F32).reshape(b, s, h, 2, 2, d // 4)
    x1, x2 = xr[..., 0, :], xr[..., 1, :]
    c, sn = cos[:, None], sin[:, None]
    out = jnp.stack([x1 * c - x2 * sn, x2 * c + x1 * sn], axis=-2)
    return out.reshape(b, s, h, d).astype(x.dtype)


def blocked_attention(q, k, v, scale):
    b, s, hq, dq = q.shape
    hkv, dv = k.shape[2], v.shape[-1]
    g = hq // hkv
    nb = s // Q_BLOCK
    qb = q.reshape(b, nb, Q_BLOCK, hkv, g, dq).transpose(1, 0, 2, 3, 4, 5)
    kf, vf = k.astype(F32), v.astype(F32)

    def one_block(q_blk):
        sc = jnp.einsum('bqhgd,bkhd->bhgqk', q_blk.astype(F32), kf) * scale
        p = jax.nn.softmax(sc, axis=-1)
        return jnp.einsum('bhgqk,bkhd->bqhgd', p, vf)

    o = lax.map(one_block, qb)
    return o.transpose(1, 0, 2, 3, 4, 5).reshape(b, s, hq, dv).astype(q.dtype)


def retention_chunkwise(q, k, v, log_gamma, strict):
    b, s, h, dk = q.shape
    dv = v.shape[-1]
    c = RET_CHUNK
    n = s // c
    qc = q.reshape(b, n, c, h, dk)
    kc = k.reshape(b, n, c, h, dk)
    vc = v.reshape(b, n, c, h, dv)
    pos = jnp.arange(c, dtype=F32)
    diff = pos[:, None] - pos[None, :]
    keep = (diff > 0) if strict else (diff >= 0)
    dmat = jnp.where(keep, jnp.exp(log_gamma[:, None, None] * jnp.maximum(diff, 0.0)), 0.0)
    scores = jnp.einsum('bnihd,bnjhd->bnhij', qc, kc) * dmat
    intra = jnp.einsum('bnhij,bnjhe->bnihe', scores, vc)
    k_dec = kc * jnp.exp(log_gamma * (c - 1.0 - pos)[:, None])[:, :, None]
    kv = jnp.einsum('bnjhd,bnjhe->nbhde', k_dec, vc)
    chunk_decay = jnp.exp(log_gamma * c)[:, None, None]

    def step(state, kv_n):
        return state * chunk_decay + kv_n, state

    _, s_prev = lax.scan(step, jnp.zeros((b, h, dk, dv), F32), kv)
    q_dec = qc * jnp.exp(log_gamma * (pos + 1.0)[:, None])[:, :, None]
    inter = jnp.einsum('bnihd,nbhde->bnihe', q_dec, s_prev)
    return (intra + inter).reshape(b, s, h, dv)


def gated_delta_chunked(q, k, v, g, beta):
    b, s, h, dk = q.shape
    dv = v.shape[-1]
    c = DELTA_CHUNK
    n = s // c

    def chunks(t):
        return jnp.moveaxis(t.reshape((b, n, c) + t.shape[2:]), 3, 2)

    qc, kc, vc, bc = chunks(q), chunks(k), chunks(v), chunks(beta)
    gc = jnp.cumsum(chunks(g), axis=-1)
    tri = jnp.tril(jnp.ones((c, c), bool))
    strict = jnp.tril(jnp.ones((c, c), bool), -1)
    diff = gc[..., :, None] - gc[..., None, :]
    lmat = jnp.where(tri, jnp.exp(jnp.where(tri, diff, 0.0)), 0.0)
    kb = kc * bc[..., None]
    a_low = jnp.where(strict, jnp.einsum('bnhid,bnhjd->bnhij', kb, kc) * lmat, 0.0)
    rhs = jnp.concatenate([vc * bc[..., None], kb * jnp.exp(gc)[..., None]], axis=-1)
    sol = lax.linalg.triangular_solve(a_low + jnp.eye(c, dtype=F32), rhs,
                                      left_side=True, lower=True, unit_diagonal=True)
    u, w = sol[..., :dv], sol[..., dv:]
    qk = jnp.einsum('bnhid,bnhjd->bnhij', qc, kc) * lmat
    q_dec = qc * jnp.exp(gc)[..., None]
    g_last = gc[..., -1]
    k_dec = kc * jnp.exp(g_last[..., None] - gc)[..., None]
    xs = tuple(jnp.moveaxis(t, 1, 0) for t in (u, w, qk, q_dec, k_dec, jnp.exp(g_last)))

    def step(state, xs_n):
        u_n, w_n, qk_n, qd_n, kd_n, dl_n = xs_n
        v_new = u_n - jnp.einsum('bhcd,bhde->bhce', w_n, state)
        o = jnp.einsum('bhcd,bhde->bhce', qd_n, state) + jnp.einsum('bhij,bhje->bhie', qk_n, v_new)
        state = state * dl_n[..., None, None] + jnp.einsum('bhcd,bhce->bhde', kd_n, v_new)
        return state, o

    _, o = lax.scan(step, jnp.zeros((b, h, dk, dv), F32), xs)
    return o.transpose(1, 0, 3, 2, 4).reshape(b, s, h, dv)


def centred_depthwise_conv(x, w):
    kw, ch = w.shape
    return lax.conv_general_dilated(x, w[:, None, :].astype(x.dtype), window_strides=(1,),
                                    padding=[(kw // 2, kw // 2)],
                                    dimension_numbers=('NWC', 'WIO', 'NWC'),
                                    feature_group_count=ch)


def flip_t(t):
    return jnp.flip(t, axis=1)


def token_mixers(h, rope64, rope32, w_in, attn_q_norm, attn_k_norm, mla_kv_norm, w_kv_up,
                 mla_q_norm, mla_k_norm, ret_decay_logit, ret_norm, conv_w, delta_a_log,
                 delta_dt_bias, delta_norm, out_norm_attn, out_norm_mla):
    b, s, _ = h.shape
    proj = h @ w_in
    (a_q, a_k, a_v, b_q, b_ckv, b_krope, c_q, c_k, c_v, c_g,
     d_q, d_k, d_v, d_z, d_a, d_b) = split_cols(proj)

    qa = apply_axial_rope(rms_norm(a_q.reshape(b, s, A_HEADS, A_HEAD_DIM), attn_q_norm), *rope64)
    ka = apply_axial_rope(rms_norm(a_k.reshape(b, s, A_KV_HEADS, A_HEAD_DIM), attn_k_norm), *rope64)
    va = a_v.reshape(b, s, A_KV_HEADS, A_HEAD_DIM)
    o_a = blocked_attention(qa, ka, va, A_HEAD_DIM ** -0.5).reshape(b, s, GROUP_W)
    o_a = rms_norm(o_a, out_norm_attn)

    kv = rms_norm(b_ckv, mla_kv_norm) @ w_kv_up
    kv = kv.reshape(b, s, B_HEADS, B_NOPE + B_V)
    k_rope = jnp.broadcast_to(b_krope[:, :, None, :], (b, s, B_HEADS, B_ROPE))
    kb_full = rms_norm(jnp.concatenate([kv[..., :B_NOPE], k_rope], axis=-1), mla_k_norm)
    qb_full = rms_norm(b_q.reshape(b, s, B_HEADS, B_NOPE + B_ROPE), mla_q_norm)
    qb_full = jnp.concatenate([qb_full[..., :B_NOPE], apply_axial_rope(qb_full[..., B_NOPE:], *rope32)], axis=-1)
    kb_full = jnp.concatenate([kb_full[..., :B_NOPE], apply_axial_rope(kb_full[..., B_NOPE:], *rope32)], axis=-1)
    o_b = blocked_attention(qb_full, kb_full, kv[..., B_NOPE:], (B_NOPE + B_ROPE) ** -0.5)
    o_b = rms_norm(o_b.reshape(b, s, GROUP_W), out_norm_mla)

    qc = apply_axial_rope(c_q.reshape(b, s, C_HEADS, C_QK), *rope32).astype(F32)
    kc = apply_axial_rope(c_k.reshape(b, s, C_HEADS, C_QK), *rope32).astype(F32) * C_QK ** -0.5
    vc = c_v.reshape(b, s, C_HEADS, C_V).astype(F32)
    log_gamma = jax.nn.log_sigmoid(ret_decay_logit.astype(F32))
    o_c = (retention_chunkwise(qc, kc, vc, log_gamma[0], False)
           + flip_t(retention_chunkwise(flip_t(qc), flip_t(kc), flip_t(vc), log_gamma[1], True)))
    o_c = rms_norm(o_c, ret_norm) * jax.nn.silu(c_g.reshape(b, s, C_HEADS, C_V).astype(F32))
    o_c = o_c.reshape(b, s, GROUP_W).astype(h.dtype)

    qkv = jax.nn.silu(centred_depthwise_conv(jnp.concatenate([d_q, d_k, d_v], axis=-1), conv_w).astype(F32))
    qd = l2_norm(qkv[..., :D_HEADS * D_K].reshape(b, s, D_HEADS, D_K)) * D_K ** -0.5
    kd = l2_norm(qkv[..., D_HEADS * D_K:2 * D_HEADS * D_K].reshape(b, s, D_HEADS, D_K))
    vd = qkv[..., 2 * D_HEADS * D_K:].reshape(b, s, D_HEADS, D_V)
    gd = -jnp.exp(delta_a_log.astype(F32)) * jax.nn.softplus(
        d_a.reshape(b, s, 2, D_HEADS).astype(F32) + delta_dt_bias.astype(F32))
    beta = jax.nn.sigmoid(d_b.reshape(b, s, 2, D_HEADS).astype(F32))
    o_d = (gated_delta_chunked(qd, kd, vd, gd[:, :, 0], beta[:, :, 0])
           + flip_t(gated_delta_chunked(flip_t(qd), flip_t(kd), flip_t(vd),
                                        flip_t(gd[:, :, 1]), flip_t(beta[:, :, 1]))))
    o_d = rms_norm(o_d, delta_norm) * jax.nn.silu(d_z.reshape(b, s, D_HEADS, D_V).astype(F32))
    o_d = o_d.reshape(b, s, GROUP_W).astype(h.dtype)

    return jnp.concatenate([o_a, o_b, o_c, o_d], axis=-1)


def swiglu(h, w1, w3, w2):
    return (jax.nn.silu(h @ w1) * (h @ w3)) @ w2


def moe_ffn(h, router_w, w1, w3, w2):
    logits = jnp.einsum('bsd,de->bse', h, router_w).astype(F32)
    top_v, top_i = lax.top_k(logits, TOP_K)
    wts = jax.nn.softmax(top_v, axis=-1)
    gates = jnp.sum(jax.nn.one_hot(top_i, N_EXPERTS, dtype=F32) * wts[..., None], axis=-2)
    y = jnp.zeros(h.shape, F32)
    for e in range(N_EXPERTS):
        y = y + gates[..., e:e + 1] * swiglu(h, w1[e], w3[e], w2[e]).astype(F32)
    return y.astype(h.dtype)


def run_trunk(x, ln_mix, w_in, attn_q_norm, attn_k_norm, mla_kv_norm, w_kv_up, mla_q_norm,
              mla_k_norm, ret_decay_logit, ret_norm, conv_w, delta_a_log, delta_dt_bias,
              delta_norm, out_norm_attn, out_norm_mla, w_out, ln_ffn, ffn_w1, ffn_w3, ffn_w2,
              router_w, moe_w1, moe_w3, moe_w2):
    n_tok = x.shape[1]
    rope64 = axial_rope_tables(n_tok, A_HEAD_DIM)
    rope32 = axial_rope_tables(n_tok, B_ROPE)
    for l in range(DEPTH):
        h = rms_norm(x, ln_mix[l])
        mix = token_mixers(h, rope64, rope32, w_in[l], attn_q_norm[l], attn_k_norm[l],
                           mla_kv_norm[l], w_kv_up[l], mla_q_norm[l], mla_k_norm[l],
                           ret_decay_logit[l], ret_norm[l], conv_w[l], delta_a_log[l],
                           delta_dt_bias[l], delta_norm[l], out_norm_attn[l], out_norm_mla[l])
        x = x + mix @ w_out[l]
        h = rms_norm(x, ln_ffn[l])
        if l % 2 == 0:
            x = x + swiglu(h, ffn_w1[l // 2], ffn_w3[l // 2], ffn_w2[l // 2])
        else:
            x = x + moe_ffn(h, router_w[l // 2], moe_w1[l // 2], moe_w3[l // 2], moe_w2[l // 2])
    return x


def setup_inputs(seed: int = 0) -> dict:
    key = jax.random.key(seed)
    ks = iter(jax.random.split(key, 40))

    def nrm(shape, fan_in):
        return jax.random.normal(next(ks), shape, F32) * fan_in ** -0.5

    def gain(shape):
        return 1.0 + 0.02 * jax.random.normal(next(ks), shape, F32)

    heads = jnp.arange(C_HEADS, dtype=F32)
    ret_logit = jnp.log(2.0 ** (5.0 + heads) - 1.0)
    ret_decay_logit = ret_logit + 0.1 * jax.random.normal(next(ks), (DEPTH, 2, C_HEADS), F32)
    delta_a_log = jnp.log(jax.random.uniform(next(ks), (DEPTH, 2, D_HEADS), F32, 1.0, 16.0))
    dt = jnp.exp(jax.random.uniform(next(ks), (DEPTH, 2, D_HEADS), F32, math.log(1e-3), math.log(1e-1)))
    delta_dt_bias = dt + jnp.log(-jnp.expm1(-dt))
    return {
        'x_prompt': jax.random.normal(next(ks), (BATCH, SEQ, D_MODEL), F32),
        'x_sample': jax.random.normal(next(ks), (DEC_BATCH, DEC_SEQ, D_MODEL), F32),
        'ln_mix': gain((DEPTH, D_MODEL)),
        'w_in': nrm((DEPTH, D_MODEL, D_IN), D_MODEL),
        'attn_q_norm': gain((DEPTH, A_HEAD_DIM)),
        'attn_k_norm': gain((DEPTH, A_HEAD_DIM)),
        'mla_kv_norm': gain((DEPTH, KV_RANK)),
        'w_kv_up': nrm((DEPTH, KV_RANK, B_HEADS * (B_NOPE + B_V)), KV_RANK),
        'mla_q_norm': gain((DEPTH, B_NOPE + B_ROPE)),
        'mla_k_norm': gain((DEPTH, B_NOPE + B_ROPE)),
        'ret_decay_logit': ret_decay_logit,
        'ret_norm': gain((DEPTH, C_HEADS, C_V)),
        'conv_w': nrm((DEPTH, CONV_K, CONV_CH), CONV_K),
        'delta_a_log': delta_a_log,
        'delta_dt_bias': delta_dt_bias,
        'delta_norm': gain((DEPTH, D_HEADS, D_V)),
        'out_norm_attn': gain((DEPTH, GROUP_W)),
        'out_norm_mla': gain((DEPTH, GROUP_W)),
        'w_out': nrm((DEPTH, D_MIX, D_MODEL), D_MIX),
        'ln_ffn': gain((DEPTH, D_MODEL)),
        'ffn_w1': nrm((N_DENSE, D_MODEL, D_FF), D_MODEL),
        'ffn_w3': nrm((N_DENSE, D_MODEL, D_FF), D_MODEL),
        'ffn_w2': nrm((N_DENSE, D_FF, D_MODEL), D_FF),
        'router_w': nrm((N_MOE, D_MODEL, N_EXPERTS), D_MODEL),
        'moe_w1': nrm((N_MOE, N_EXPERTS, D_MODEL, D_FF_EXPERT), D_MODEL),
        'moe_w3': nrm((N_MOE, N_EXPERTS, D_MODEL, D_FF_EXPERT), D_MODEL),
        'moe_w2': nrm((N_MOE, N_EXPERTS, D_FF_EXPERT, D_MODEL), D_FF_EXPERT),
    }


def reference(x_prompt, x_sample, ln_mix, w_in, attn_q_norm, attn_k_norm, mla_kv_norm, w_kv_up,
              mla_q_norm, mla_k_norm, ret_decay_logit, ret_norm, conv_w, delta_a_log,
              delta_dt_bias, delta_norm, out_norm_attn, out_norm_mla, w_out, ln_ffn, ffn_w1,
              ffn_w3, ffn_w2, router_w, moe_w1, moe_w3, moe_w2):
    y_prompt = run_trunk(x_prompt, ln_mix, w_in, attn_q_norm, attn_k_norm, mla_kv_norm, w_kv_up,
                         mla_q_norm, mla_k_norm, ret_decay_logit, ret_norm, conv_w, delta_a_log,
                         delta_dt_bias, delta_norm, out_norm_attn, out_norm_mla, w_out, ln_ffn,
                         ffn_w1, ffn_w3, ffn_w2, router_w, moe_w1, moe_w3, moe_w2)
    y_sample = run_trunk(x_sample, ln_mix, w_in, attn_q_norm, attn_k_norm, mla_kv_norm, w_kv_up,
                         mla_q_norm, mla_k_norm, ret_decay_logit, ret_norm, conv_w, delta_a_log,
                         delta_dt_bias, delta_norm, out_norm_attn, out_norm_mla, w_out, ln_ffn,
                         ffn_w1, ffn_w3, ffn_w2, router_w, moe_w1, moe_w3, moe_w2)
    return (y_prompt, y_sample)
```

```python
import functools
import math

import jax
import jax.numpy as jnp
from jax import lax
from jax.experimental import pallas as pl
from jax.experimental.pallas import tpu as pltpu

F32 = jnp.float32
BF16 = jnp.bfloat16

D_MODEL = 1024
DEPTH = 2
GRID_W = 64
ROPE_THETA = 10000.0
EPS = 1e-6
GROUP_W = 256
A_HEADS, A_KV_HEADS, A_HEAD_DIM = 4, 2, 64
B_HEADS, B_NOPE, B_ROPE, B_V, KV_RANK = 4, 64, 32, 64, 128
C_HEADS, C_QK, C_V, RET_CHUNK = 4, 32, 64, 128
D_HEADS, D_K, D_V, CONV_K, DELTA_CHUNK = 4, 64, 64, 5, 64
D_FF = 3584
N_EXPERTS, TOP_K = 8, 2
SPLIT_SIZES = (256, 128, 128, 384, 128, 32, 128, 128, 256, 256, 256, 256, 256, 256, 8, 8)

LANES = 128
VMEM_LIMIT = 56 * 1024 * 1024

W_PA = 512
W_PB = 768
W_PC = 768
W_PD = 1152
W_IN = W_PA + W_PB + W_PC + W_PD


def _cparams(sem):
    return pltpu.CompilerParams(dimension_semantics=sem, vmem_limit_bytes=VMEM_LIMIT)


def _nt(a, b):
    return lax.dot_general(a, b, (((1,), (1,)), ((), ())), preferred_element_type=F32)


def _tn(a, b):
    return lax.dot_general(a, b, (((0,), (0,)), ((), ())), preferred_element_type=F32)


def _dot(a, b):
    return jnp.dot(a, b, preferred_element_type=F32)


def _dot_exact(a, b):
    return jnp.dot(a, b, preferred_element_type=F32, precision=lax.Precision.HIGHEST)


def _same_group(shape, gsize):
    sh = int(math.log2(gsize))
    r = lax.shift_right_logical(lax.broadcasted_iota(jnp.int32, shape, 0), sh)
    c = lax.shift_right_logical(lax.broadcasted_iota(jnp.int32, shape, 1), sh)
    return r == c


def _group_sum(x, gsize):
    w = x.shape[-1]
    ones = _same_group((w, w), gsize).astype(F32)
    return _dot_exact(x, ones)


def _rope(x, cos, sin_signed, half):
    w = x.shape[-1]
    lane = lax.broadcasted_iota(jnp.int32, x.shape, 1)
    hi = (lane & half) != 0
    partner = jnp.where(hi, pltpu.roll(x, half, 1), pltpu.roll(x, w - half, 1))
    return x * cos + partner * sin_signed


def _silu(x):
    return x * (1.0 / (1.0 + jnp.exp(-x)))


def _inproj_kernel(x_ref, g_ref, w_ref, pa_ref, pb_ref, pc_ref, pd_ref):
    x = x_ref[...]
    ms = jnp.mean(x * x, axis=-1, keepdims=True)
    h = (x * lax.rsqrt(ms + EPS) * g_ref[...]).astype(BF16)
    off = 0
    for ref, w in ((pa_ref, W_PA), (pb_ref, W_PB), (pc_ref, W_PC), (pd_ref, W_PD)):
        ref[...] = _dot(h, w_ref[:, off:off + w])
        off += w


def _inproj(x2, gain, w_in, tm=512):
    t = x2.shape[0]
    row = lambda w: pl.BlockSpec((tm, w), lambda i: (i, 0))
    return pl.pallas_call(
        _inproj_kernel,
        grid=(t // tm,),
        in_specs=[row(D_MODEL), pl.BlockSpec((1, D_MODEL), lambda i: (0, 0)),
                  pl.BlockSpec((D_MODEL, W_IN), lambda i: (0, 0))],
        out_specs=[row(W_PA), row(W_PB), row(W_PC), row(W_PD)],
        out_shape=[jax.ShapeDtypeStruct((t, w), F32) for w in (W_PA, W_PB, W_PC, W_PD)],
        compiler_params=_cparams(("parallel",)),
        name="inproj",
    )(x2, gain, w_in)


def _aprep_kernel(pa_ref, cos_ref, sin_ref, gq_ref, gk_ref, q_ref, k_ref, ksw_ref, v_ref, vsw_ref):
    pa = pa_ref[...]
    q = pa[:, 0:256]
    k = pa[:, 256:384]
    v = pa[:, 384:512]
    cos = cos_ref[...]
    sin = sin_ref[...]
    qn = q * lax.rsqrt(_group_sum(q * q, A_HEAD_DIM) * (1.0 / A_HEAD_DIM) + EPS) * gq_ref[...]
    kn = k * lax.rsqrt(_group_sum(k * k, A_HEAD_DIM) * (1.0 / A_HEAD_DIM) + EPS) * gk_ref[...]
    q_ref[...] = _rope(qn, cos, sin, 16).astype(BF16)
    kr = _rope(kn, cos[:, 0:128], sin[:, 0:128], 16)
    k_ref[...] = kr.astype(BF16)
    ksw_ref[...] = pltpu.roll(kr, 64, 1).astype(BF16)
    v_ref[...] = v.astype(BF16)
    vsw_ref[...] = pltpu.roll(v, 64, 1).astype(BF16)


def _aprep(pa, cos, sin, gq, gk, s, tm=512):
    t = pa.shape[0]
    tm = min(tm, s)
    nb = s // tm
    row = lambda w: pl.BlockSpec((tm, w), lambda i: (i, 0))
    tab = lambda w: pl.BlockSpec((tm, w), lambda i: (i % nb, 0))
    vec = lambda w: pl.BlockSpec((1, w), lambda i: (0, 0))
    return pl.pallas_call(
        _aprep_kernel,
        grid=(t // tm,),
        in_specs=[row(W_PA), tab(256), tab(256), vec(256), vec(128)],
        out_specs=[row(256), row(128), row(128), row(128), row(128)],
        out_shape=[jax.ShapeDtypeStruct((t, w), BF16) for w in (256, 128, 128, 128, 128)],
        compiler_params=_cparams(("parallel",)),
        name="attn_a_prep",
    )(pa, cos, sin, gq, gk)


def _bprep_kernel(pb_ref, cos_ref, sin_ref, gc_ref, gq_ref, gk_ref, wkv_ref, q_ref, k_ref, v_ref):
    pb = pb_ref[...]
    cos = cos_ref[...]
    sin = sin_ref[...]
    ckv = pb[:, 512:640]
    krope = pb[:, 640:768]
    cn = (ckv * lax.rsqrt(jnp.mean(ckv * ckv, axis=-1, keepdims=True) + EPS) * gc_ref[...]).astype(BF16)
    kvp = _dot(cn, wkv_ref[...])
    inv_d = 1.0 / (B_NOPE + B_ROPE)
    for h in range(B_HEADS):
        sl = slice(128 * h, 128 * (h + 1))
        kp = kvp[:, sl] + krope
        kn = kp * lax.rsqrt(jnp.sum(kp * kp, axis=-1, keepdims=True) * inv_d + EPS) * gk_ref[...]
        k_ref[:, sl] = _rope(kn, cos, sin, 8).astype(BF16)
        qh = pb[:, sl]
        qn = qh * lax.rsqrt(jnp.sum(qh * qh, axis=-1, keepdims=True) * inv_d + EPS) * gq_ref[...]
        q_ref[:, sl] = _rope(qn, cos, sin, 8).astype(BF16)
    v_ref[...] = kvp[:, 512:768].astype(BF16)


def _bprep(pb, cos, sin, gc, gq, gk, wkv, s, tm=512):
    t = pb.shape[0]
    tm = min(tm, s)
    nb = s // tm
    row = lambda w: pl.BlockSpec((tm, w), lambda i: (i, 0))
    tab = lambda w: pl.BlockSpec((tm, w), lambda i: (i % nb, 0))
    vec = lambda w: pl.BlockSpec((1, w), lambda i: (0, 0))
    return pl.pallas_call(
        _bprep_kernel,
        grid=(t // tm,),
        in_specs=[row(W_PB), tab(128), tab(128), vec(128), vec(128), vec(128),
                  pl.BlockSpec((KV_RANK, 768), lambda i: (0, 0))],
        out_specs=[row(512), row(512), row(256)],
        out_shape=[jax.ShapeDtypeStruct((t, w), BF16) for w in (512, 512, 256)],
        compiler_params=_cparams(("parallel",)),
        name="attn_b_prep",
    )(pb, cos, sin, gc, gq, gk, wkv)


_HEADS_A = ((0, 0, 0, 0, 0, 0, 0, 0), (0, 1, 1, 0, 1, 0, 0, 1),
            (1, 0, 1, 0, 1, 0, 1, 0), (1, 1, 0, 0, 0, 0, 1, 1))
_HEADS_B = ((0, None, 0, 0, 0, 0, 0, 0), (1, None, 0, 1, 0, 0, 0, 1),
            (2, None, 0, 2, 0, 1, 1, 0), (3, None, 0, 3, 0, 1, 1, 1))


def _attn_kernel(*refs, heads, n_k, n_v):
    q_ref = refs[0]
    k_refs = refs[1:1 + n_k]
    v_refs = refs[1 + n_k:1 + n_k + n_v]
    g_ref = refs[1 + n_k + n_v]
    o_ref = refs[2 + n_k + n_v]
    tq = q_ref.shape[0]
    lo = lax.broadcasted_iota(jnp.int32, (tq, LANES), 1) < 64
    outs = [None, None]
    for qs, qh, ki, ks, vi, vs, osl, oh in heads:
        q = q_ref[:, 128 * qs:128 * (qs + 1)]
        if qh is not None:
            q = jnp.where(lo if qh == 0 else jnp.logical_not(lo), q, jnp.zeros_like(q))
        k = k_refs[ki][:, 128 * ks:128 * (ks + 1)]
        s = _nt(q, k)
        m = jnp.max(s, axis=-1, keepdims=True)
        p = jnp.exp(s - m)
        l = jnp.sum(p, axis=-1, keepdims=True)
        v = v_refs[vi][:, 128 * vs:128 * (vs + 1)]
        o = _dot(p.astype(BF16), v) / l
        if outs[osl] is None:
            outs[osl] = o
        else:
            keep = lo if oh == 0 else jnp.logical_not(lo)
            outs[osl] = jnp.where(keep, o, outs[osl])
    o = jnp.concatenate(outs, axis=1)
    ms = jnp.mean(o * o, axis=-1, keepdims=True)
    o_ref[...] = (o * lax.rsqrt(ms + EPS) * g_ref[...]).astype(BF16)


def _attention(q, ks, vs, gain, heads, s, tq=256):
    t = q.shape[0]
    tq = min(tq, s)
    nq = s // tq
    nb = t // s
    qspec = pl.BlockSpec((tq, q.shape[1]), lambda b, i: (b * nq + i, 0))
    kv = lambda a: pl.BlockSpec((s, a.shape[1]), lambda b, i: (b, 0))
    return pl.pallas_call(
        functools.partial(_attn_kernel, heads=heads, n_k=len(ks), n_v=len(vs)),
        grid=(nb, nq),
        in_specs=[qspec] + [kv(a) for a in ks] + [kv(a) for a in vs]
                 + [pl.BlockSpec((1, GROUP_W), lambda b, i: (0, 0))],
        out_specs=pl.BlockSpec((tq, GROUP_W), lambda b, i: (b * nq + i, 0)),
        out_shape=jax.ShapeDtypeStruct((t, GROUP_W), BF16),
        compiler_params=_cparams(("parallel", "arbitrary")),
        name="attention",
    )(q, *ks, *vs, gain)


def _ret_kernel(pc_ref, cos_ref, sin_ref, dmat_ref, dec_ref, gst_ref, gain_ref, o_ref,
                q_scr, k_scr, kvf_scr, kvb_scr):
    s = pc_ref.shape[0]
    c = RET_CHUNK
    n = s // c
    cos = cos_ref[...]
    sin = sin_ref[...]
    q_scr[...] = _rope(pc_ref[:, 0:128], cos, sin, 8)
    k_scr[...] = _rope(pc_ref[:, 128:256], cos, sin, 8) * (C_QK ** -0.5)
    qdf = dec_ref[0]
    kdf = dec_ref[1]
    qdb = dec_ref[2]
    kdb = dec_ref[3]
    r = lax.shift_right_logical(lax.broadcasted_iota(jnp.int32, (128, 256), 0), 5)
    cc = lax.shift_right_logical(lax.broadcasted_iota(jnp.int32, (128, 256), 1), 6)
    bd = r == cc

    def kv_body(i, carry):
        rows = pl.ds(pl.multiple_of(i * c, c), c)
        kc = k_scr[rows, :]
        vc = pc_ref[rows, 256:512].astype(BF16)
        kvf_scr[i] = jnp.where(bd, _tn((kc * kdf).astype(BF16), vc), 0.0)
        kvb_scr[i] = jnp.where(bd, _tn((kc * kdb).astype(BF16), vc), 0.0)
        return carry

    lax.fori_loop(0, n, kv_body, 0)

    gf = gst_ref[0]
    gb = gst_ref[1]

    def scan_f(i, st):
        cur = kvf_scr[i]
        kvf_scr[i] = st
        return st * gf + cur

    lax.fori_loop(0, n, scan_f, jnp.zeros((128, 256), F32))

    def scan_b(j, st):
        i = n - 1 - j
        cur = kvb_scr[i]
        kvb_scr[i] = st
        return st * gb + cur

    lax.fori_loop(0, n, scan_b, jnp.zeros((128, 256), F32))

    lane_q = lax.shift_right_logical(lax.broadcasted_iota(jnp.int32, (c, 128), 1), 5)
    lane_v = lax.shift_right_logical(lax.broadcasted_iota(jnp.int32, (c, 256), 1), 6)
    gain = gain_ref[...]

    def out_body(i, carry):
        rows = pl.ds(pl.multiple_of(i * c, c), c)
        qc = q_scr[rows, :]
        kc = k_scr[rows, :].astype(BF16)
        vc = pc_ref[rows, 256:512].astype(BF16)
        o = _dot((qc * qdf).astype(BF16), kvf_scr[i].astype(BF16))
        o = o + _dot((qc * qdb).astype(BF16), kvb_scr[i].astype(BF16))
        for h in range(C_HEADS):
            qh = jnp.where(lane_q == h, qc, 0.0).astype(BF16)
            sc = _nt(qh, kc) * dmat_ref[h]
            o = o + jnp.where(lane_v == h, _dot(sc.astype(BF16), vc), 0.0)
        ms = _group_sum(o * o, C_V) * (1.0 / C_V)
        gate = pc_ref[rows, 512:768]
        o_ref[rows, :] = (o * lax.rsqrt(ms + EPS) * gain * _silu(gate)).astype(BF16)
        return carry

    lax.fori_loop(0, n, out_body, 0)


def _retention(pc, cos, sin, dmat, dec, gst, gain, s):
    t = pc.shape[0]
    nb = t // s
    n = s // RET_CHUNK
    full = lambda a: pl.BlockSpec(a.shape, lambda b: (0,) * a.ndim)
    return pl.pallas_call(
        _ret_kernel,
        grid=(nb,),
        in_specs=[pl.BlockSpec((s, W_PC), lambda b: (b, 0)), full(cos), full(sin), full(dmat),
                  full(dec), full(gst), full(gain)],
        out_specs=pl.BlockSpec((s, GROUP_W), lambda b: (b, 0)),
        out_shape=jax.ShapeDtypeStruct((t, GROUP_W), BF16),
        scratch_shapes=[pltpu.VMEM((s, 128), F32), pltpu.VMEM((s, 128), F32),
                        pltpu.VMEM((n, 128, 256), F32), pltpu.VMEM((n, 128, 256), F32)],
        compiler_params=_cparams(("parallel",)),
        name="retention",
    )(pc, cos, sin, dmat, dec, gst, gain)


def _delta_kernel(pd_ref, cw_ref, coef_ref, gain_ref, o_ref,
                  xp_scr, q_scr, k_scr, v_scr, g_scr, b_scr, acc_scr):
    s = pd_ref.shape[0]
    c = DELTA_CHUNK
    n = s // c
    pad = 8

    xp_scr[0:pad, :] = jnp.zeros((pad, 768), F32)
    xp_scr[pad + s:pad + s + pad, :] = jnp.zeros((pad, 768), F32)
    rb = min(s, 256)
    for r0 in range(0, s, rb):
        xp_scr[pad + r0:pad + r0 + rb, :] = pd_ref[r0:r0 + rb, 0:768]
    src = lax.broadcasted_iota(jnp.int32, (128, 256), 0)
    head = lax.shift_right_logical(lax.broadcasted_iota(jnp.int32, (128, 256), 1), 6)
    for r0 in range(0, s, rb):
        y = jnp.zeros((rb, 768), F32)
        for kk in range(CONV_K):
            st = pad + r0 + kk - CONV_K // 2
            y = y + xp_scr[st:st + rb, :] * cw_ref[kk:kk + 1, :]
        y = _silu(y)
        q = y[:, 0:256]
        k = y[:, 256:512]
        q_scr[r0:r0 + rb, :] = q * lax.rsqrt(_group_sum(q * q, D_K) + EPS) * (D_K ** -0.5)
        k_scr[r0:r0 + rb, :] = k * lax.rsqrt(_group_sum(k * k, D_K) + EPS)
        v_scr[r0:r0 + rb, :] = y[:, 512:768]
        ab = pd_ref[r0:r0 + rb, 1024:1152]
        z = ab + coef_ref[1:2, :]
        softplus = jnp.maximum(z, 0.0) + jnp.log1p(jnp.exp(-jnp.abs(z)))
        gd = coef_ref[0:1, :] * softplus
        beta = 1.0 / (1.0 + jnp.exp(-ab))
        for d in range(2):
            g_scr[d, r0:r0 + rb, :] = _dot_exact(gd, (src == 4 * d + head).astype(F32))
            b_scr[d, r0:r0 + rb, :] = _dot_exact(beta, (src == 8 + 4 * d + head).astype(F32))
        acc_scr[r0:r0 + rb, :] = jnp.zeros((rb, 256), F32)

    bd = _same_group((256, 256), 64)
    ri = lax.broadcasted_iota(jnp.int32, (c, 256), 0)
    ci = lax.broadcasted_iota(jnp.int32, (c, 256), 1) & (c - 1)
    eye_cat = ri == ci
    tri_r = lax.broadcasted_iota(jnp.int32, (c, c), 0)
    tri_c = lax.broadcasted_iota(jnp.int32, (c, c), 1)
    eye_bd = lax.broadcasted_iota(jnp.int32, (256, 256), 0) == lax.broadcasted_iota(jnp.int32, (256, 256), 1)

    def tile4(a):
        return jnp.concatenate([a, a, a, a], axis=0)

    def chunk_step(d, i, st):
        rows = pl.ds(pl.multiple_of(i * c, c), c)
        qd = q_scr[rows, :]
        kd = k_scr[rows, :]
        vd = v_scr[rows, :]
        graw = g_scr[d, rows, :]
        beta_e = b_scr[d, rows, :]
        if d == 0:
            incl = ci <= ri
            strict = ci < ri
            tri = (tri_c <= tri_r).astype(F32)
        else:
            incl = ci >= ri
            strict = ci > ri
            tri = (tri_c >= tri_r).astype(F32)
        gce = _dot_exact(tri, graw)
        total = jnp.sum(graw, axis=0, keepdims=True)
        eg = jnp.exp(gce)
        ek = jnp.exp(total - gce)
        kb = kd * beta_e
        gcol = jnp.sum(jnp.where(eye_cat, gce, 0.0), axis=0, keepdims=True)
        lmat = jnp.where(incl, jnp.exp(jnp.where(incl, gce - gcol, 0.0)), 0.0)
        kbd = jnp.where(bd, tile4(kd), 0.0).astype(BF16)
        kq = _nt(jnp.concatenate([kb, qd], axis=0).astype(BF16), kbd)
        a_cat = jnp.where(strict, kq[0:c] * lmat, 0.0)
        qk = (kq[c:2 * c] * lmat).astype(BF16)
        a_bd = jnp.where(bd, tile4(a_cat), 0.0)
        t_bd = jnp.where(eye_bd, 1.0, 0.0) - a_bd
        pw = a_bd.astype(BF16)
        for _ in range(5):
            pwf = _dot(pw, pw)
            pw = pwf.astype(BF16)
            t_bd = t_bd + _dot(t_bd.astype(BF16), pw)
        t_cat = (t_bd[0:c] + t_bd[c:2 * c] + t_bd[2 * c:3 * c] + t_bd[3 * c:4 * c]).astype(BF16)
        rhs = jnp.concatenate([jnp.where(bd, tile4(vd * beta_e), 0.0),
                               jnp.where(bd, tile4(kb * eg), 0.0)], axis=1).astype(BF16)
        uw = _dot(t_cat, rhs)
        u = uw[:, 0:256]
        w = uw[:, 256:512]
        sq = _dot(jnp.concatenate([w, qd * eg], axis=0).astype(BF16), st.astype(BF16))
        v_new = u - sq[0:c]
        vbd = jnp.where(bd, tile4(v_new), 0.0).astype(BF16)
        o = sq[c:2 * c] + _dot(qk, vbd)
        acc_scr[rows, :] += o
        upd = _tn((kd * ek).astype(BF16), v_new.astype(BF16))
        return st * jnp.exp(total) + jnp.where(bd, upd, 0.0)

    def body(i, carry):
        sf, sb = carry
        sf = chunk_step(0, i, sf)
        sb = chunk_step(1, n - 1 - i, sb)
        return sf, sb

    zero = jnp.zeros((256, 256), F32)
    lax.fori_loop(0, n, body, (zero, zero))

    for r0 in range(0, s, rb):
        o = acc_scr[r0:r0 + rb, :]
        ms = _group_sum(o * o, D_V) * (1.0 / D_V)
        zg = pd_ref[r0:r0 + rb, 768:1024]
        o_ref[r0:r0 + rb, :] = (o * lax.rsqrt(ms + EPS) * gain_ref[...] * _silu(zg)).astype(BF16)


def _delta(pd, conv_w, coef, gain, s):
    t = pd.shape[0]
    nb = t // s
    full = lambda a: pl.BlockSpec(a.shape, lambda b: (0,) * a.ndim)
    return pl.pallas_call(
        _delta_kernel,
        grid=(nb,),
        in_specs=[pl.BlockSpec((s, W_PD), lambda b: (b, 0)), full(conv_w), full(coef), full(gain)],
        out_specs=pl.BlockSpec((s, GROUP_W), lambda b: (b, 0)),
        out_shape=jax.ShapeDtypeStruct((t, GROUP_W), BF16),
        scratch_shapes=[pltpu.VMEM((s + 16, 768), F32), pltpu.VMEM((s, 256), F32),
                        pltpu.VMEM((s, 256), F32), pltpu.VMEM((s, 256), F32),
                        pltpu.VMEM((2, s, 256), F32), pltpu.VMEM((2, s, 256), F32),
                        pltpu.VMEM((s, 256), F32)],
        compiler_params=_cparams(("parallel",)),
        name="deltanet",
    )(pd, conv_w, coef, gain)


def _outproj_kernel(x_ref, oa_ref, ob_ref, oc_ref, od_ref, w_ref, y_ref):
    acc = x_ref[...]
    for g, ref in enumerate((oa_ref, ob_ref, oc_ref, od_ref)):
        acc = acc + _dot(ref[...], w_ref[GROUP_W * g:GROUP_W * (g + 1), :])
    y_ref[...] = acc


def _outproj(x2, oa, ob, oc, od, w_out, tm=512):
    t = x2.shape[0]
    row = lambda w: pl.BlockSpec((tm, w), lambda i: (i, 0))
    return pl.pallas_call(
        _outproj_kernel,
        grid=(t // tm,),
        in_specs=[row(D_MODEL)] + [row(GROUP_W)] * 4 + [pl.BlockSpec((D_MODEL, D_MODEL), lambda i: (0, 0))],
        out_specs=row(D_MODEL),
        out_shape=jax.ShapeDtypeStruct((t, D_MODEL), F32),
        compiler_params=_cparams(("parallel",)),
        name="outproj",
    )(x2, oa, ob, oc, od, w_out)


FF_CHUNK = 512


def _ffn_kernel(x_ref, g_ref, w1_ref, w3_ref, w2_ref, y_ref, h_scr, acc_scr):
    j = pl.program_id(1)

    @pl.when(j == 0)
    def _():
        x = x_ref[...]
        ms = jnp.mean(x * x, axis=-1, keepdims=True)
        h_scr[...] = (x * lax.rsqrt(ms + EPS) * g_ref[...]).astype(BF16)
        acc_scr[...] = x

    h = h_scr[...]
    a = _dot(h, w1_ref[...])
    b = _dot(h, w3_ref[...])
    acc_scr[...] += _dot((_silu(a) * b).astype(BF16), w2_ref[...])

    @pl.when(j == pl.num_programs(1) - 1)
    def _():
        y_ref[...] = acc_scr[...]


def _ffn(x2, gain, w1, w3, w2, tm=1024):
    t = x2.shape[0]
    tm = min(tm, t)
    nj = D_FF // FF_CHUNK
    return pl.pallas_call(
        _ffn_kernel,
        grid=(t // tm, nj),
        in_specs=[pl.BlockSpec((tm, D_MODEL), lambda i, j: (i, 0)),
                  pl.BlockSpec((1, D_MODEL), lambda i, j: (0, 0)),
                  pl.BlockSpec((D_MODEL, FF_CHUNK), lambda i, j: (0, j)),
                  pl.BlockSpec((D_MODEL, FF_CHUNK), lambda i, j: (0, j)),
                  pl.BlockSpec((FF_CHUNK, D_MODEL), lambda i, j: (j, 0))],
        out_specs=pl.BlockSpec((tm, D_MODEL), lambda i, j: (i, 0)),
        out_shape=jax.ShapeDtypeStruct((t, D_MODEL), F32),
        scratch_shapes=[pltpu.VMEM((tm, D_MODEL), BF16), pltpu.VMEM((tm, D_MODEL), F32)],
        compiler_params=_cparams(("parallel", "arbitrary")),
        name="ffn_dense",
    )(x2, gain, w1, w3, w2)


def _moe_kernel(x_ref, g_ref, wr_ref, w1_ref, w3_ref, w2_ref, y_ref, h_scr, acc_scr, gate_scr):
    e = pl.program_id(1)
    j = pl.program_id(2)

    @pl.when((e == 0) & (j == 0))
    def _():
        x = x_ref[...]
        ms = jnp.mean(x * x, axis=-1, keepdims=True)
        h = x * lax.rsqrt(ms + EPS) * g_ref[...]
        h_scr[...] = h.astype(BF16)
        acc_scr[...] = x
        logits = _dot_exact(h, wr_ref[...])
        lane = lax.broadcasted_iota(jnp.int32, logits.shape, 1)
        neg = jnp.float32(-jnp.inf)
        lg = jnp.where(lane < N_EXPERTS, logits, neg)
        m1 = jnp.max(lg, axis=-1, keepdims=True)
        i1 = jnp.min(jnp.where(lg == m1, lane, LANES), axis=-1, keepdims=True)
        lg2 = jnp.where(lane == i1, neg, lg)
        m2 = jnp.max(lg2, axis=-1, keepdims=True)
        i2 = jnp.min(jnp.where(lg2 == m2, lane, LANES), axis=-1, keepdims=True)
        e2 = jnp.exp(m2 - m1)
        w_top = 1.0 / (1.0 + e2)
        gate_scr[...] = jnp.where(lane == i1, w_top, 0.0) + jnp.where(lane == i2, e2 * w_top, 0.0)

    h = h_scr[...]
    a = _dot(h, w1_ref[0])
    b = _dot(h, w3_ref[0])
    y = _dot((_silu(a) * b).astype(BF16), w2_ref[0])
    lane = lax.broadcasted_iota(jnp.int32, gate_scr.shape, 1)
    gate = jnp.sum(jnp.where(lane == e, gate_scr[...], 0.0), axis=-1, keepdims=True)
    acc_scr[...] += gate * y

    @pl.when((e == pl.num_programs(1) - 1) & (j == pl.num_programs(2) - 1))
    def _():
        y_ref[...] = acc_scr[...]


def _moe(x2, gain, wr, w1, w3, w2, tm=1024):
    t = x2.shape[0]
    tm = min(tm, t)
    nj = D_FF // FF_CHUNK
    return pl.pallas_call(
        _moe_kernel,
        grid=(t // tm, N_EXPERTS, nj),
        in_specs=[pl.BlockSpec((tm, D_MODEL), lambda i, e, j: (i, 0)),
                  pl.BlockSpec((1, D_MODEL), lambda i, e, j: (0, 0)),
                  pl.BlockSpec((D_MODEL, LANES), lambda i, e, j: (0, 0)),
                  pl.BlockSpec((1, D_MODEL, FF_CHUNK), lambda i, e, j: (e, 0, j)),
                  pl.BlockSpec((1, D_MODEL, FF_CHUNK), lambda i, e, j: (e, 0, j)),
                  pl.BlockSpec((1, FF_CHUNK, D_MODEL), lambda i, e, j: (e, j, 0))],
        out_specs=pl.BlockSpec((tm, D_MODEL), lambda i, e, j: (i, 0)),
        out_shape=jax.ShapeDtypeStruct((t, D_MODEL), F32),
        scratch_shapes=[pltpu.VMEM((tm, D_MODEL), BF16), pltpu.VMEM((tm, D_MODEL), F32),
                        pltpu.VMEM((tm, LANES), F32)],
        compiler_params=_cparams(("parallel", "arbitrary", "arbitrary")),
        name="ffn_moe",
    )(x2, gain, wr, w1, w3, w2)


def _rope_tables(n_tokens, d_rot):
    rows = n_tokens // GRID_W
    n_f = d_rot // 4
    freqs = ROPE_THETA ** (-jnp.arange(n_f, dtype=F32) / n_f)
    row = jnp.repeat(jnp.arange(rows, dtype=F32), GRID_W)
    col = jnp.tile(jnp.arange(GRID_W, dtype=F32), rows)
    ar = row[:, None] * freqs
    ac = col[:, None] * freqs
    cos = jnp.concatenate([jnp.cos(ar), jnp.cos(ar), jnp.cos(ac), jnp.cos(ac)], axis=1)
    sin = jnp.concatenate([-jnp.sin(ar), jnp.sin(ar), -jnp.sin(ac), jnp.sin(ac)], axis=1)
    return cos, sin


def _layout_w_in(w):
    parts, off = [], 0
    for size in SPLIT_SIZES:
        parts.append(w[:, off:off + size])
        off += size
    (a_q, a_k, a_v, b_q, b_ckv, b_kr, c_q, c_k, c_v, c_g, d_q, d_k, d_v, d_z, d_a, d_b) = parts
    z = lambda n: jnp.zeros((D_MODEL, n), w.dtype)
    dq = B_NOPE + B_ROPE
    bq = [blk for h in range(B_HEADS) for blk in (b_q[:, dq * h:dq * (h + 1)], z(128 - dq))]
    cols = [a_q, a_k, a_v] + bq + [b_ckv, z(64), b_kr, z(32), c_q, c_k, c_v, c_g,
                                   d_q, d_k, d_v, d_z, d_a, d_b, z(112)]
    return jnp.concatenate(cols, axis=1).astype(BF16)


def _layout_w_kv(w):
    z = jnp.zeros((KV_RANK, 64), w.dtype)
    ks, vs = [], []
    for h in range(B_HEADS):
        blk = w[:, 128 * h:128 * (h + 1)]
        ks += [blk[:, 0:64], z]
        vs.append(blk[:, 64:128])
    return jnp.concatenate(ks + vs, axis=1).astype(BF16)


def _pad_lanes(v, n):
    return jnp.concatenate([v, jnp.zeros((n - v.shape[0],), v.dtype)])[None, :]


def _retention_tables(logit):
    lg = jax.nn.log_sigmoid(logit.astype(F32))
    c = RET_CHUNK
    pos = jnp.arange(c, dtype=F32)
    diff = pos[:, None] - pos[None, :]
    dmat = jnp.where(diff >= 0, jnp.exp(lg[0][:, None, None] * jnp.maximum(diff, 0.0)),
                     jnp.exp(lg[1][:, None, None] * jnp.maximum(-diff, 0.0)))
    ex = lambda v: jnp.repeat(v, C_QK)[None, :]
    dec = jnp.stack([jnp.exp(ex(lg[0]) * (pos + 1.0)[:, None]),
                     jnp.exp(ex(lg[0]) * (c - 1.0 - pos)[:, None]),
                     jnp.exp(ex(lg[1]) * (c - pos)[:, None]),
                     jnp.exp(ex(lg[1]) * pos[:, None])])
    st = lambda v: jnp.broadcast_to(jnp.exp(jnp.repeat(v, C_QK) * c)[:, None], (128, 256))
    gst = jnp.stack([st(lg[0]), st(lg[1])])
    return dmat, dec, gst


def _run_trunk(x, p):
    b, s, _ = x.shape
    t = b * s
    x2 = x.reshape(t, D_MODEL)
    cos64, sin64 = _rope_tables(s, A_HEAD_DIM)
    cos32, sin32 = _rope_tables(s, B_ROPE)
    cos_a = jnp.tile(cos64, (1, 4))
    sin_a = jnp.tile(sin64, (1, 4))
    ones = lambda n: jnp.ones((s, n), F32)
    zeros = lambda n: jnp.zeros((s, n), F32)
    cos_b = jnp.concatenate([ones(64), cos32, ones(32)], axis=1)
    sin_b = jnp.concatenate([zeros(64), sin32, zeros(32)], axis=1)
    cos_c = jnp.tile(cos32, (1, 4))
    sin_c = jnp.tile(sin32, (1, 4))
    for l in range(DEPTH):
        w_in = _layout_w_in(p['w_in'][l])
        pa, pb, pc, pd = _inproj(x2, p['ln_mix'][l][None, :], w_in)
        gq = (jnp.tile(p['attn_q_norm'][l], 4) * A_HEAD_DIM ** -0.5)[None, :]
        gk = jnp.tile(p['attn_k_norm'][l], 2)[None, :]
        qa, ka, ka_sw, va, va_sw = _aprep(pa, cos_a, sin_a, gq, gk, s)
        oa = _attention(qa, (ka, ka_sw), (va, va_sw), p['out_norm_attn'][l][None, :], _HEADS_A, s)
        dq = B_NOPE + B_ROPE
        qb, kb, vb = _bprep(pb, cos_b, sin_b, p['mla_kv_norm'][l][None, :],
                            _pad_lanes(p['mla_q_norm'][l] * dq ** -0.5, 128),
                            _pad_lanes(p['mla_k_norm'][l], 128), _layout_w_kv(p['w_kv_up'][l]), s)
        ob = _attention(qb, (kb,), (vb,), p['out_norm_mla'][l][None, :], _HEADS_B, s)
        dmat, dec, gst = _retention_tables(p['ret_decay_logit'][l])
        oc = _retention(pc, cos_c, sin_c, dmat, dec, gst, p['ret_norm'][l].reshape(1, GROUP_W), s)
        coef = jnp.concatenate([_pad_lanes(-jnp.exp(p['delta_a_log'][l].astype(F32)).reshape(-1), 128),
                                _pad_lanes(p['delta_dt_bias'][l].astype(F32).reshape(-1), 128)], axis=0)
        od = _delta(pd, p['conv_w'][l], coef, p['delta_norm'][l].reshape(1, GROUP_W), s)
        x2 = _outproj(x2, oa, ob, oc, od, p['w_out'][l].astype(BF16))
        if l % 2 == 0:
            i = l // 2
            x2 = _ffn(x2, p['ln_ffn'][l][None, :], p['ffn_w1'][i].astype(BF16),
                      p['ffn_w3'][i].astype(BF16), p['ffn_w2'][i].astype(BF16))
        else:
            i = l // 2
            wr = jnp.concatenate([p['router_w'][i], jnp.zeros((D_MODEL, LANES - N_EXPERTS), F32)], axis=1)
            x2 = _moe(x2, p['ln_ffn'][l][None, :], wr, p['moe_w1'][i].astype(BF16),
                      p['moe_w3'][i].astype(BF16), p['moe_w2'][i].astype(BF16))
    return x2.reshape(b, s, D_MODEL)


def kernel(x_prompt, x_sample, ln_mix, w_in, attn_q_norm, attn_k_norm, mla_kv_norm, w_kv_up, mla_q_norm, mla_k_norm, ret_decay_logit, ret_norm, conv_w, delta_a_log, delta_dt_bias, delta_norm, out_norm_attn, out_norm_mla, w_out, ln_ffn, ffn_w1, ffn_w3, ffn_w2, router_w, moe_w1, moe_w3, moe_w2):
    p = dict(ln_mix=ln_mix, w_in=w_in, attn_q_norm=attn_q_norm, attn_k_norm=attn_k_norm,
             mla_kv_norm=mla_kv_norm, w_kv_up=w_kv_up, mla_q_norm=mla_q_norm, mla_k_norm=mla_k_norm,
             ret_decay_logit=ret_decay_logit, ret_norm=ret_norm, conv_w=conv_w, delta_a_log=delta_a_log,
             delta_dt_bias=delta_dt_bias, delta_norm=delta_norm, out_norm_attn=out_norm_attn,
             out_norm_mla=out_norm_mla, w_out=w_out, ln_ffn=ln_ffn, ffn_w1=ffn_w1, ffn_w3=ffn_w3,
             ffn_w2=ffn_w2, router_w=router_w, moe_w1=moe_w1, moe_w3=moe_w3, moe_w2=moe_w2)
    return (_run_trunk(x_prompt, p), _run_trunk(x_sample, p))
```
